```python
import math
import jax, jax.numpy as jnp
from jax import lax
import numpy as np

D_MODEL = 1024
BATCH = 2
SEQ = 8192
DEPTH = 2
DEC_BATCH = 128
DEC_SEQ = 4
PAST_LEN = 8192
PAGE_SIZE = 128

HEAD_DIM = 64
BLOCK = 128
ROPE_THETA = 10000.0
LN_EPS = 1e-5
SSM_WIDTH = D_MODEL // 4
SSM_GROUP = 16
SSM_GROUPS = SSM_WIDTH // SSM_GROUP
SSM_STATE = 64
DIL_WIDTH = 3 * D_MODEL // 8
DIL_HEADS = DIL_WIDTH // HEAD_DIM
DIL_PAIRS = ((128, 1), (512, 4), (2048, 16))
DIL_REACH = 2048
SWA_WIDTH = 3 * D_MODEL // 8
SWA_HEADS = SWA_WIDTH // HEAD_DIM
SWA_KV_HEADS = 2
SWA_GROUP = SWA_HEADS // SWA_KV_HEADS
SWA_KV_WIDTH = SWA_KV_HEADS * HEAD_DIM
SWA_WINDOW = 128
MIX_WIDTH = SSM_WIDTH + DIL_WIDTH + SWA_WIDTH
IN_COLS = SSM_WIDTH + 3 * DIL_WIDTH + SWA_WIDTH + 2 * SWA_KV_WIDTH
SPLITS = (SSM_WIDTH, SSM_WIDTH + DIL_WIDTH, SSM_WIDTH + 2 * DIL_WIDTH, SSM_WIDTH + 3 * DIL_WIDTH,
          SSM_WIDTH + 3 * DIL_WIDTH + SWA_WIDTH, SSM_WIDTH + 3 * DIL_WIDTH + SWA_WIDTH + SWA_KV_WIDTH)
MEM_TOKENS = 256
MEM_HEADS = 4
MEM_HEAD_DIM = D_MODEL // MEM_HEADS
N_GROUPS = 4
EXPERTS_PER_GROUP = 8
N_EXPERTS = N_GROUPS * EXPERTS_PER_GROUP
TOP_K_IN_GROUP = 2
EXPERT_FF = 512
DEEPNORM_ALPHA = (2 * DEPTH) ** 0.25
DEEPNORM_BETA = (8 * DEPTH) ** -0.25

kernel_name = 'hymba_s5_longnet_swa_hmoe_step'


def _layer_norm(x, g, b):
    xf = x.astype(jnp.float32)
    mu = jnp.mean(xf, -1, keepdims=True)
    var = jnp.mean(jnp.square(xf - mu), -1, keepdims=True)
    return ((xf - mu) * lax.rsqrt(var + LN_EPS) * g.astype(jnp.float32) + b.astype(jnp.float32)).astype(x.dtype)


def _rope(x, pos):
    half = x.shape[-1] // 2
    inv_freq = ROPE_THETA ** (-jnp.arange(half, dtype=jnp.float32) / half)
    ang = pos.astype(jnp.float32)[:, None] * inv_freq[None, :]
    bshape = (1, pos.shape[0]) + (1,) * (x.ndim - 3) + (half,)
    cos = jnp.cos(ang).reshape(bshape)
    sin = jnp.sin(ang).reshape(bshape)
    xf = x.astype(jnp.float32)
    x1, x2 = xf[..., :half], xf[..., half:]
    return jnp.concatenate([x1 * cos - x2 * sin, x2 * cos + x1 * sin], -1).astype(x.dtype)


def _banded_attention(q, k, v, window):
    n, L, hk, g, hd = q.shape
    nb = -(-L // BLOCK)
    pad = nb * BLOCK - L
    qb = jnp.pad(q, ((0, 0), (0, pad), (0, 0), (0, 0), (0, 0))).reshape(n, nb, BLOCK, hk, g, hd)
    kp = jnp.pad(k, ((0, 0), (BLOCK, pad), (0, 0), (0, 0))).reshape(n, nb + 1, BLOCK, hk, hd)
    vp = jnp.pad(v, ((0, 0), (BLOCK, pad), (0, 0), (0, 0))).reshape(n, nb + 1, BLOCK, hk, hd)
    kk = jnp.concatenate([kp[:, :-1], kp[:, 1:]], axis=2)
    vv = jnp.concatenate([vp[:, :-1], vp[:, 1:]], axis=2)
    s = jnp.einsum('nbqhgd,nbkhd->nbhgqk', qb, kk, preferred_element_type=jnp.float32) * (hd ** -0.5)
    qi = jnp.arange(BLOCK)[:, None] + BLOCK
    ki = jnp.arange(2 * BLOCK)[None, :]
    dist = qi - ki
    kpos = jnp.arange(nb)[:, None, None] * BLOCK - BLOCK + ki[None]
    valid = (dist >= 0) & (dist <= window) & (kpos >= 0)
    s = jnp.where(valid[None, :, None, None], s, -jnp.inf)
    m = jnp.max(s, -1, keepdims=True)
    p = jnp.exp(s - m)
    den = jnp.sum(p, -1, keepdims=True)
    o = jnp.einsum('nbhgqk,nbkhd->nbqhgd', p / den, vv.astype(jnp.float32))
    lse = (m + jnp.log(den))[..., 0].transpose(0, 1, 4, 2, 3)
    return (o.reshape(n, nb * BLOCK, hk, g, hd)[:, :L], lse.reshape(n, nb * BLOCK, hk, g)[:, :L])


def _dilated_prompt(q, k, v, window, dil):
    n, S = q.shape[:2]

    def split(t):
        t = t.reshape((n, S // dil, dil) + t.shape[2:])
        t = jnp.moveaxis(t, 2, 1)
        return t.reshape((n * dil, S // dil) + t.shape[3:])

    def merge(t):
        t = t.reshape((n, dil, S // dil) + t.shape[2:])
        t = jnp.moveaxis(t, 1, 2)
        return t.reshape((n, S) + t.shape[3:])

    o, lse = _banded_attention(split(q), split(k), split(v), window // dil)
    return merge(o), merge(lse)


def _gathered_window(q, k_all, v_all, window, dil, past):
    t_new = q.shape[1]
    n_keys = window // dil + 1
    idx = past + jnp.arange(t_new)[:, None] - dil * jnp.arange(n_keys)[None, :]
    valid = idx >= 0
    idx = jnp.maximum(idx, 0)
    kg = k_all[:, idx]
    vg = v_all[:, idx]
    s = jnp.einsum('nthgd,ntkhd->nthgk', q, kg, preferred_element_type=jnp.float32) * (q.shape[-1] ** -0.5)
    s = jnp.where(valid[None, :, None, None, :], s, -jnp.inf)
    m = jnp.max(s, -1, keepdims=True)
    p = jnp.exp(s - m)
    den = jnp.sum(p, -1, keepdims=True)
    o = jnp.einsum('nthgk,ntkhd->nthgd', p / den, vg.astype(jnp.float32))
    return o, (m + jnp.log(den))[..., 0]


def _s5(u, h0_re, h0_im, w):
    f32 = jnp.float32
    u = u.astype(f32)
    lam_re = w['ssm_lambda_re'].astype(f32)
    lam_im = w['ssm_lambda_im'].astype(f32)
    dt = jnp.exp(w['ssm_log_dt'].astype(f32))[:, None]
    mag = jnp.exp(lam_re * dt)
    ar = mag * jnp.cos(lam_im * dt)
    ai = mag * jnp.sin(lam_im * dt)
    den = lam_re * lam_re + lam_im * lam_im
    fr = ((ar - 1.0) * lam_re + ai * lam_im) / den
    fi = (ai * lam_re - (ar - 1.0) * lam_im) / den
    b_re = w['ssm_b_re'].astype(f32)
    b_im = w['ssm_b_im'].astype(f32)
    bb_re = fr[..., None] * b_re - fi[..., None] * b_im
    bb_im = fr[..., None] * b_im + fi[..., None] * b_re
    bu_re = jnp.einsum('nlgh,gph->nlgp', u, bb_re)
    bu_im = jnp.einsum('nlgh,gph->nlgp', u, bb_im)
    a_re = jnp.broadcast_to(ar, bu_re.shape)
    a_im = jnp.broadcast_to(ai, bu_im.shape)

    def combine(e1, e2):
        a1r, a1i, b1r, b1i = e1
        a2r, a2i, b2r, b2i = e2
        return (a2r * a1r - a2i * a1i, a2r * a1i + a2i * a1r,
                a2r * b1r - a2i * b1i + b2r, a2r * b1i + a2i * b1r + b2i)

    cum_re, cum_im, h_re, h_im = lax.associative_scan(combine, (a_re, a_im, bu_re, bu_im), axis=1)
    s_re = h0_re.astype(f32)[:, None]
    s_im = h0_im.astype(f32)[:, None]
    h_re, h_im = h_re + cum_re * s_re - cum_im * s_im, h_im + cum_re * s_im + cum_im * s_re
    y = (jnp.einsum('nlgp,ghp->nlgh', h_re, w['ssm_c_re'].astype(f32))
         - jnp.einsum('nlgp,ghp->nlgh', h_im, w['ssm_c_im'].astype(f32))
         + w['ssm_d'].astype(f32) * u)
    return y, h_re[:, -1], h_im[:, -1]


def _memory_attention(x, mem_k, mem_v, wq, wo):
    n, L, _ = x.shape
    q = (x @ wq).reshape(n, L, MEM_HEADS, MEM_HEAD_DIM)
    s = jnp.einsum('nlhd,nmhd->nhlm', q, mem_k.astype(q.dtype), preferred_element_type=jnp.float32) * (MEM_HEAD_DIM ** -0.5)
    p = jax.nn.softmax(s, axis=-1)
    o = jnp.einsum('nhlm,nmhd->nlhd', p, mem_v.astype(jnp.float32))
    return o.reshape(n, L, D_MODEL).astype(x.dtype) @ wo


def _hier_moe(x, w):
    f32 = jnp.float32
    n, L, d = x.shape
    xt = x.reshape(n * L, d)
    g_logit = jnp.dot(xt, w['w_route_group'], preferred_element_type=f32) + w['b_route_group'].astype(f32)
    g_prob = jax.nn.softmax(g_logit, axis=-1)
    g_sel = jnp.argmax(g_logit, axis=-1)
    g_wt = jnp.take_along_axis(g_prob, g_sel[:, None], axis=1)
    e_logit = (jnp.dot(xt, w['w_route_expert'], preferred_element_type=f32)
               + w['b_route_expert'].astype(f32)).reshape(-1, N_GROUPS, EXPERTS_PER_GROUP)
    e_logit = jnp.take_along_axis(e_logit, g_sel[:, None, None], axis=1)[:, 0]
    top_v, top_i = lax.top_k(e_logit, TOP_K_IN_GROUP)
    e_wt = jax.nn.softmax(top_v, axis=-1) * g_wt
    e_id = g_sel[:, None] * EXPERTS_PER_GROUP + top_i
    gates = jnp.sum(jax.nn.one_hot(e_id, N_EXPERTS, dtype=f32) * e_wt[..., None], axis=1).astype(x.dtype)
    out = jnp.zeros_like(xt)
    for e in range(N_EXPERTS):
        hid = jax.nn.silu(xt @ w['w_exp_gate'][e]) * (xt @ w['w_exp_up'][e])
        out = out + gates[:, e:e + 1] * (hid @ w['w_exp_down'][e])
    return out.reshape(n, L, d)


def _layer(x, pos, mem_k, mem_v, past, w):
    n, L, _ = x.shape
    h = x @ w['w_in']
    u, qb, kb, vb, qc, kc, vc = jnp.split(h, SPLITS, axis=-1)
    qb = _rope(qb.reshape(n, L, DIL_HEADS, HEAD_DIM), pos)[:, :, :, None]
    kb = _rope(kb.reshape(n, L, DIL_HEADS, HEAD_DIM), pos)
    vb = vb.reshape(n, L, DIL_HEADS, HEAD_DIM)
    qc = _rope(qc.reshape(n, L, SWA_KV_HEADS, SWA_GROUP, HEAD_DIM), pos)
    kc = _rope(kc.reshape(n, L, SWA_KV_HEADS, HEAD_DIM), pos)
    vc = vc.reshape(n, L, SWA_KV_HEADS, HEAD_DIM)
    if past is None:
        branches = [_dilated_prompt(qb, kb, vb, win, dil) for win, dil in DIL_PAIRS]
        oc, lc = _banded_attention(qc, kc, vc, SWA_WINDOW)
        keep_b = min(DIL_REACH, L)
        keep_c = min(SWA_WINDOW, L)
        new_bk, new_bv = kb[:, L - keep_b:], vb[:, L - keep_b:]
        new_ck, new_cv = kc[:, L - keep_c:], vc[:, L - keep_c:]
        h0_re = jnp.zeros((n, SSM_GROUPS, SSM_STATE), jnp.float32)
        h0_im = jnp.zeros((n, SSM_GROUPS, SSM_STATE), jnp.float32)
    else:
        dk, dv, sk, sv, h0_re, h0_im = past
        kb_all = jnp.concatenate([dk.astype(kb.dtype), kb], axis=1)
        vb_all = jnp.concatenate([dv.astype(vb.dtype), vb], axis=1)
        branches = [_gathered_window(qb, kb_all, vb_all, win, dil, dk.shape[1]) for win, dil in DIL_PAIRS]
        kc_all = jnp.concatenate([sk.astype(kc.dtype), kc], axis=1)
        vc_all = jnp.concatenate([sv.astype(vc.dtype), vc], axis=1)
        oc, lc = _gathered_window(qc, kc_all, vc_all, SWA_WINDOW, 1, sk.shape[1])
        new_bk, new_bv, new_ck, new_cv = kb, vb, kc, vc
    o_br = jnp.stack([o for o, _ in branches])
    l_br = jnp.stack([l for _, l in branches])
    wts = jnp.exp(l_br - jax.nn.logsumexp(l_br, axis=0, keepdims=True))
    ob = jnp.sum(wts[..., None] * o_br, axis=0)
    sink = w['swa_sink'].astype(jnp.float32).reshape(SWA_KV_HEADS, SWA_GROUP)
    oc = oc * jax.nn.sigmoid(lc - sink)[..., None]
    ya, st_re, st_im = _s5(u.reshape(n, L, SSM_GROUPS, SSM_GROUP), h0_re, h0_im, w)
    za = jax.nn.gelu(ya.reshape(n, L, SSM_WIDTH).astype(x.dtype))
    ga = za @ w['w_glu']
    a_out = ga[..., :SSM_WIDTH] * jax.nn.sigmoid(ga[..., SSM_WIDTH:])
    mix = jnp.concatenate([a_out, ob.reshape(n, L, DIL_WIDTH).astype(x.dtype),
                           oc.reshape(n, L, SWA_WIDTH).astype(x.dtype)], axis=-1) @ w['w_out']
    x = _layer_norm(DEEPNORM_ALPHA * x + mix, w['ln_gain'][0], w['ln_bias'][0])
    x = _layer_norm(DEEPNORM_ALPHA * x + _memory_attention(x, mem_k, mem_v, w['w_mem_q'], w['w_mem_o']),
                    w['ln_gain'][1], w['ln_bias'][1])
    x = _layer_norm(DEEPNORM_ALPHA * x + _hier_moe(x, w), w['ln_gain'][2], w['ln_bias'][2])
    return x, new_bk, new_bv, new_ck, new_cv, st_re, st_im


def setup_inputs(seed: int = 0) -> dict:
    key = jax.random.key(seed)
    keys = iter(jax.random.split(key, 40))
    f32 = jnp.float32

    def nrm(shape, scale):
        return scale * jax.random.normal(next(keys), shape, f32)

    lb = min(DIL_REACH, PAST_LEN)
    lc = min(SWA_WINDOW, PAST_LEN)
    n_idx = jnp.arange(SSM_STATE, dtype=f32)
    return {
        'x_prompt': nrm((BATCH, SEQ, D_MODEL), 1.0),
        'x_sample': nrm((DEC_BATCH, DEC_SEQ, D_MODEL), 1.0),
        'mem_prompt': nrm((BATCH, MEM_TOKENS, D_MODEL), 1.0),
        'cache_dil_k': nrm((DEPTH, DEC_BATCH, lb, DIL_HEADS, HEAD_DIM), 1.0),
        'cache_dil_v': nrm((DEPTH, DEC_BATCH, lb, DIL_HEADS, HEAD_DIM), 1.0),
        'cache_swa_k': nrm((DEPTH, DEC_BATCH, lc, SWA_KV_HEADS, HEAD_DIM), 1.0),
        'cache_swa_v': nrm((DEPTH, DEC_BATCH, lc, SWA_KV_HEADS, HEAD_DIM), 1.0),
        'state_ssm_re': nrm((DEPTH, DEC_BATCH, SSM_GROUPS, SSM_STATE), 0.5),
        'state_ssm_im': nrm((DEPTH, DEC_BATCH, SSM_GROUPS, SSM_STATE), 0.5),
        'cache_mem_k': nrm((DEPTH, DEC_BATCH, MEM_TOKENS, MEM_HEADS, MEM_HEAD_DIM), 1.0),
        'cache_mem_v': nrm((DEPTH, DEC_BATCH, MEM_TOKENS, MEM_HEADS, MEM_HEAD_DIM), 1.0),
        'w_in': nrm((DEPTH, D_MODEL, IN_COLS), D_MODEL ** -0.5),
        'ssm_lambda_re': -0.5 + nrm((DEPTH, SSM_GROUPS, SSM_STATE), 0.01),
        'ssm_lambda_im': math.pi * n_idx + nrm((DEPTH, SSM_GROUPS, SSM_STATE), 0.01),
        'ssm_b_re': nrm((DEPTH, SSM_GROUPS, SSM_STATE, SSM_GROUP), (2 * SSM_GROUP) ** -0.5),
        'ssm_b_im': nrm((DEPTH, SSM_GROUPS, SSM_STATE, SSM_GROUP), (2 * SSM_GROUP) ** -0.5),
        'ssm_c_re': nrm((DEPTH, SSM_GROUPS, SSM_GROUP, SSM_STATE), (2 * SSM_STATE) ** -0.5),
        'ssm_c_im': nrm((DEPTH, SSM_GROUPS, SSM_GROUP, SSM_STATE), (2 * SSM_STATE) ** -0.5),
        'ssm_d': nrm((DEPTH, SSM_GROUPS, SSM_GROUP), 1.0),
        'ssm_log_dt': jax.random.uniform(next(keys), (DEPTH, SSM_GROUPS), f32, math.log(1e-3), math.log(1e-1)),
        'w_glu': nrm((DEPTH, SSM_WIDTH, 2 * SSM_WIDTH), SSM_WIDTH ** -0.5),
        'swa_sink': nrm((DEPTH, SWA_HEADS), 1.0),
        'w_out': nrm((DEPTH, MIX_WIDTH, D_MODEL), DEEPNORM_BETA * MIX_WIDTH ** -0.5),
        'w_mem_q': nrm((DEPTH, D_MODEL, D_MODEL), D_MODEL ** -0.5),
        'w_mem_k': nrm((DEPTH, D_MODEL, D_MODEL), D_MODEL ** -0.5),
        'w_mem_v': nrm((DEPTH, D_MODEL, D_MODEL), D_MODEL ** -0.5),
        'w_mem_o': nrm((DEPTH, D_MODEL, D_MODEL), DEEPNORM_BETA * D_MODEL ** -0.5),
        'ln_gain': 1.0 + nrm((DEPTH, 3, D_MODEL), 0.02),
        'ln_bias': nrm((DEPTH, 3, D_MODEL), 0.02),
        'w_route_group': nrm((DEPTH, D_MODEL, N_GROUPS), D_MODEL ** -0.5),
        'b_route_group': nrm((DEPTH, N_GROUPS), 0.01),
        'w_route_expert': nrm((DEPTH, D_MODEL, N_EXPERTS), D_MODEL ** -0.5),
        'b_route_expert': nrm((DEPTH, N_EXPERTS), 0.01),
        'w_exp_gate': nrm((DEPTH, N_EXPERTS, D_MODEL, EXPERT_FF), D_MODEL ** -0.5),
        'w_exp_up': nrm((DEPTH, N_EXPERTS, D_MODEL, EXPERT_FF), D_MODEL ** -0.5),
        'w_exp_down': nrm((DEPTH, N_EXPERTS, EXPERT_FF, D_MODEL), DEEPNORM_BETA * EXPERT_FF ** -0.5),
    }


def reference(x_prompt, x_sample, mem_prompt, cache_dil_k, cache_dil_v, cache_swa_k, cache_swa_v,
              state_ssm_re, state_ssm_im, cache_mem_k, cache_mem_v,
              w_in, ssm_lambda_re, ssm_lambda_im, ssm_b_re, ssm_b_im, ssm_c_re, ssm_c_im, ssm_d, ssm_log_dt,
              w_glu, swa_sink, w_out, w_mem_q, w_mem_k, w_mem_v, w_mem_o, ln_gain, ln_bias,
              w_route_group, b_route_group, w_route_expert, b_route_expert, w_exp_gate, w_exp_up, w_exp_down):
    pos_p = jnp.arange(x_prompt.shape[1])
    pos_s = PAST_LEN + jnp.arange(x_sample.shape[1])
    n_p = mem_prompt.shape[0]
    yp, ys = x_prompt, x_sample
    p_bk, p_bv, p_ck, p_cv, p_sr, p_si, p_mk, p_mv = [], [], [], [], [], [], [], []
    s_bk, s_bv, s_ck, s_cv, s_sr, s_si = [], [], [], [], [], []
    for l in range(DEPTH):
        w = {'w_in': w_in[l], 'ssm_lambda_re': ssm_lambda_re[l], 'ssm_lambda_im': ssm_lambda_im[l],
             'ssm_b_re': ssm_b_re[l], 'ssm_b_im': ssm_b_im[l], 'ssm_c_re': ssm_c_re[l], 'ssm_c_im': ssm_c_im[l],
             'ssm_d': ssm_d[l], 'ssm_log_dt': ssm_log_dt[l], 'w_glu': w_glu[l], 'swa_sink': swa_sink[l],
             'w_out': w_out[l], 'w_mem_q': w_mem_q[l], 'w_mem_o': w_mem_o[l],
             'ln_gain': ln_gain[l], 'ln_bias': ln_bias[l],
             'w_route_group': w_route_group[l], 'b_route_group': b_route_group[l],
             'w_route_expert': w_route_expert[l], 'b_route_expert': b_route_expert[l],
             'w_exp_gate': w_exp_gate[l], 'w_exp_up': w_exp_up[l], 'w_exp_down': w_exp_down[l]}
        mk = (mem_prompt @ w_mem_k[l]).reshape(n_p, MEM_TOKENS, MEM_HEADS, MEM_HEAD_DIM)
        mv = (mem_prompt @ w_mem_v[l]).reshape(n_p, MEM_TOKENS, MEM_HEADS, MEM_HEAD_DIM)
        yp, bk, bv, ck, cv, sr, si = _layer(yp, pos_p, mk, mv, None, w)
        p_bk.append(bk); p_bv.append(bv); p_ck.append(ck); p_cv.append(cv)
        p_sr.append(sr); p_si.append(si); p_mk.append(mk); p_mv.append(mv)
        past = (cache_dil_k[l], cache_dil_v[l], cache_swa_k[l], cache_swa_v[l], state_ssm_re[l], state_ssm_im[l])
        ys, bk, bv, ck, cv, sr, si = _layer(ys, pos_s, cache_mem_k[l], cache_mem_v[l], past, w)
        s_bk.append(bk); s_bv.append(bv); s_ck.append(ck); s_cv.append(cv)
        s_sr.append(sr); s_si.append(si)
    return (yp, ys,
            jnp.stack(p_bk), jnp.stack(p_bv), jnp.stack(p_ck), jnp.stack(p_cv),
            jnp.stack(p_sr), jnp.stack(p_si), jnp.stack(p_mk), jnp.stack(p_mv),
            jnp.stack(s_bk), jnp.stack(s_bv), jnp.stack(s_ck), jnp.stack(s_cv),
            jnp.stack(s_sr), jnp.stack(s_si))
```

```python
import functools
import math

import numpy as np
import jax
import jax.numpy as jnp
from jax import lax
from jax.experimental import pallas as pl
from jax.experimental.pallas import tpu as pltpu

F32, BF16, I32 = jnp.float32, jnp.bfloat16, jnp.int32

D_MODEL = 1024
HEAD_DIM = 64
ROPE_THETA = 10000.0
LN_EPS = 1e-5
PAST_LEN = 8192
DEPTH = 2
ALPHA = (2 * DEPTH) ** 0.25
SSM_WIDTH = 256
SSM_GROUP = 16
SSM_GROUPS = 16
SSM_STATE = 64
SSM_LANES = SSM_GROUPS * SSM_STATE
ATT_WIDTH = 384
ATT_HEADS = 6
SWA_KV_WIDTH = 128
BAND = 128
DILATIONS = (1, 4, 16)
DIL_REACH = 2048
SWA_WINDOW = 128
MEM_TOKENS = 256
MEM_HEADS = 4
MEM_HEAD_DIM = 256
N_GROUPS = 4
EXPERTS_PER_GROUP = 8
N_EXPERTS = 32
EXPERT_FF = 512
IN_COLS = 2048
LANES = 128
SUBLANES = 8
VMEM_LIMIT = 56 * 1024 * 1024
NEG = -1e30

SSM_CHUNK = 256
SSM_SEG = SSM_CHUNK // SUBLANES


def _cp(sem, vmem=VMEM_LIMIT):
    return pltpu.CompilerParams(dimension_semantics=sem, vmem_limit_bytes=vmem)


def _layer_norm(z, g, b):
    mu = jnp.mean(z, -1, keepdims=True)
    zc = z - mu
    var = jnp.mean(zc * zc, -1, keepdims=True)
    return zc * lax.rsqrt(var + LN_EPS) * g + b


def _nt(a, b):
    return lax.dot_general(a, b, (((1,), (1,)), ((), ())), preferred_element_type=F32)


def _dot(a, b):
    return jnp.dot(a, b, preferred_element_type=F32)


def _inproj_kernel(x_ref, w_ref, cos_ref, sin_ref, u_ref, qb_ref, kb_ref, vb_ref, qc_ref, kc_ref, vc_ref):
    h = _dot(x_ref[...].astype(BF16), w_ref[...])
    cos = cos_ref[...]
    sin = sin_ref[...]
    lane = lax.broadcasted_iota(I32, cos.shape, 1)
    first_half = (lane % HEAD_DIM) < (HEAD_DIM // 2)

    def rope(seg, scale):
        outs = []
        for c in range(seg.shape[1] // LANES):
            s = seg[:, c * LANES:(c + 1) * LANES]
            partner = jnp.where(first_half, pltpu.roll(s, LANES - HEAD_DIM // 2, 1), pltpu.roll(s, HEAD_DIM // 2, 1))
            outs.append((s * cos + partner * sin) * scale)
        return jnp.concatenate(outs, axis=1)

    q_scale = HEAD_DIM ** -0.5
    u_ref[...] = h[:, 0:256]
    qb_ref[...] = rope(h[:, 256:640], q_scale)
    kb_ref[...] = rope(h[:, 640:1024], 1.0)
    vb_ref[...] = h[:, 1024:1408]
    qc_ref[...] = rope(h[:, 1408:1792], q_scale)
    kc_ref[...] = rope(h[:, 1792:1920], 1.0)
    vc_ref[...] = h[:, 1920:2048]


def _inproj(x, w_in, cos, sin, tm):
    T = x.shape[0]
    row = lambda w: pl.BlockSpec((tm, w), lambda i: (i, 0))
    widths = (256, 384, 384, 384, 384, 128, 128)
    return pl.pallas_call(
        _inproj_kernel,
        grid=(T // tm,),
        in_specs=[row(D_MODEL), pl.BlockSpec((D_MODEL, IN_COLS), lambda i: (0, 0)), row(LANES), row(LANES)],
        out_specs=[row(w) for w in widths],
        out_shape=[jax.ShapeDtypeStruct((T, w), F32) for w in widths],
        compiler_params=_cp(("parallel",)),
        name="inproj",
    )(x, w_in, cos, sin)


def _ssm_prep_kernel(lre_r, lim_r, ldt_r, lre_c, lim_c, ldt_c, bre_ref, bim_ref,
                     ar_ref, ai_ref, pwr_ref, pwi_ref, bbr_ref, bbi_ref):
    def disc(lre, lim, ldt):
        dt = jnp.exp(ldt)
        mag = jnp.exp(lre * dt)
        ar = mag * jnp.cos(lim * dt)
        ai = mag * jnp.sin(lim * dt)
        den = lre * lre + lim * lim
        fr = ((ar - 1.0) * lre + ai * lim) / den
        fi = (ai * lre - (ar - 1.0) * lim) / den
        return ar, ai, fr, fi

    ar, ai, _, _ = disc(lre_r[...], lim_r[...], ldt_r[...])
    ar_ref[...] = ar
    ai_ref[...] = ai
    pr, pi = ar, ai
    for i in range(SSM_SEG):
        pwr_ref[pl.ds(i, 1), :] = pr
        pwi_ref[pl.ds(i, 1), :] = pi
        pr, pi = pr * ar - pi * ai, pr * ai + pi * ar
    _, _, fr, fi = disc(lre_c[...], lim_c[...], ldt_c[...])
    bre = bre_ref[...]
    bim = bim_ref[...]
    bbr_ref[...] = fr * bre - fi * bim
    bbi_ref[...] = fr * bim + fi * bre


def _ssm_prep(lam_re, lam_im, log_dt, b_re, b_im):
    n = SSM_LANES
    ldt = jnp.repeat(log_dt, SSM_STATE)
    rows = [a.reshape(1, n) for a in (lam_re, lam_im, ldt)]
    cols = [a.reshape(n, 1) for a in (lam_re, lam_im, ldt)]
    full = lambda s: pl.BlockSpec(s, lambda: tuple(0 for _ in s))
    outs = pl.pallas_call(
        _ssm_prep_kernel,
        in_specs=[full((1, n))] * 3 + [full((n, 1))] * 3 + [full((n, SSM_GROUP))] * 2,
        out_specs=[full((1, n))] * 2 + [full((SSM_SEG, n))] * 2 + [full((n, SSM_GROUP))] * 2,
        out_shape=[jax.ShapeDtypeStruct((1, n), F32)] * 2 + [jax.ShapeDtypeStruct((SSM_SEG, n), F32)] * 2
        + [jax.ShapeDtypeStruct((n, SSM_GROUP), F32)] * 2,
        name="ssm_prep",
    )(*rows, *cols, b_re.reshape(n, SSM_GROUP), b_im.reshape(n, SSM_GROUP))
    ar, ai, pwr, pwi, bbr, bbi = outs
    eye = jnp.eye(SSM_GROUPS, dtype=F32)

    def blockdiag_in(bb):
        t = bb.reshape(SSM_GROUPS, SSM_STATE, SSM_GROUP).transpose(0, 2, 1)
        return (t[:, :, None, :] * eye[:, None, :, None]).reshape(SSM_WIDTH, n).astype(BF16)

    return ar, ai, pwr, pwi, blockdiag_in(bbr), blockdiag_in(bbi)


def _blockdiag_out(c):
    eye = jnp.eye(SSM_GROUPS, dtype=F32)
    t = c.transpose(0, 2, 1)
    return (t[:, :, None, :] * eye[:, None, :, None]).reshape(SSM_LANES, SSM_WIDTH)


def _ssm_tail(hr, hi, u, cc_ref, d_ref):
    hcat = jnp.concatenate([hr.astype(BF16), hi.astype(BF16)], axis=1)
    return _dot(hcat, cc_ref[...]) + d_ref[...] * u


def _glu(za, wglu_ref):
    ga = _dot(za, wglu_ref[...])
    return ga[:, :SSM_WIDTH] * jax.nn.sigmoid(ga[:, SSM_WIDTH:])


def _ssm_prompt_kernel(u_ref, pm_ref, pmt_ref, bbr_ref, bbi_ref, ar_ref, ai_ref, pwr_ref, pwi_ref,
                       cc_ref, d_ref, wglu_ref, a_ref, sr_ref, si_ref, hr_s, hi_s, cr_s, ci_s):
    c = pl.program_id(1)

    @pl.when(c == 0)
    def _():
        cr_s[...] = jnp.zeros_like(cr_s)
        ci_s[...] = jnp.zeros_like(ci_s)

    u = u_ref[...]
    u1 = u.astype(BF16)
    r1 = u - u1.astype(F32)
    u2 = r1.astype(BF16)
    u3 = (r1 - u2.astype(F32)).astype(BF16)
    up = _dot(pm_ref[...], jnp.concatenate([u1, u2, u3], axis=1))
    w = SSM_WIDTH
    u_p = (up[:, :w] + up[:, w:2 * w]) + up[:, 2 * w:]
    ub = u_p.astype(BF16)
    hr_s[...] = _dot(ub, bbr_ref[...])
    hi_s[...] = _dot(ub, bbi_ref[...])

    shape8 = (SUBLANES, SSM_LANES)
    ar = jnp.broadcast_to(ar_ref[...], shape8)
    ai = jnp.broadcast_to(ai_ref[...], shape8)

    def local(i, carry):
        hr, hi = carry
        rows = pl.ds(pl.multiple_of(i * SUBLANES, SUBLANES), SUBLANES)
        nr = ar * hr - ai * hi + hr_s[rows, :]
        ni = ar * hi + ai * hr + hi_s[rows, :]
        hr_s[rows, :] = nr
        hi_s[rows, :] = ni
        return nr, ni

    zero8 = jnp.zeros(shape8, F32)
    er, ei = lax.fori_loop(0, SSM_SEG, local, (zero8, zero8))

    a32r = pwr_ref[pl.ds(SSM_SEG - 1, 1), :]
    a32i = pwi_ref[pl.ds(SSM_SEG - 1, 1), :]
    sub = lax.broadcasted_iota(I32, shape8, 0)
    cr, ci = cr_s[...], ci_s[...]
    st_r, st_i = zero8, zero8
    for j in range(SUBLANES):
        st_r = jnp.where(sub == j, cr, st_r)
        st_i = jnp.where(sub == j, ci, st_i)
        ejr, eji = er[j:j + 1, :], ei[j:j + 1, :]
        cr, ci = ejr + a32r * cr - a32i * ci, eji + a32r * ci + a32i * cr
    cr_s[...] = cr
    ci_s[...] = ci
    sr_ref[0] = cr
    si_ref[0] = ci

    def fixup(i, _):
        rows = pl.ds(pl.multiple_of(i * SUBLANES, SUBLANES), SUBLANES)
        pr = pwr_ref[pl.ds(i, 1), :]
        pi = pwi_ref[pl.ds(i, 1), :]
        hr_s[rows, :] = hr_s[rows, :] + (pr * st_r - pi * st_i)
        hi_s[rows, :] = hi_s[rows, :] + (pr * st_i + pi * st_r)
        return 0

    lax.fori_loop(0, SSM_SEG, fixup, 0)

    y = _ssm_tail(hr_s[...], hi_s[...], u_p, cc_ref, d_ref)
    za = jax.nn.gelu(y).astype(BF16)
    za = _dot(pmt_ref[...], za).astype(BF16)
    a_ref[...] = _glu(za, wglu_ref)


def _ssm_prompt(u, consts, n_seq, seq):
    pm, pmt, bbr, bbi, ar, ai, pwr, pwi, cc, d, wglu = consts
    nch = seq // SSM_CHUNK
    full = lambda a: pl.BlockSpec(a.shape, lambda n, c: tuple(0 for _ in a.shape))
    st_spec = pl.BlockSpec((1, 1, SSM_LANES), lambda n, c: (n, 0, 0))
    return pl.pallas_call(
        _ssm_prompt_kernel,
        grid=(n_seq, nch),
        in_specs=[pl.BlockSpec((SSM_CHUNK, SSM_WIDTH), lambda n, c: (n * nch + c, 0))]
        + [full(a) for a in (pm, pmt, bbr, bbi, ar, ai, pwr, pwi, cc, d, wglu)],
        out_specs=[pl.BlockSpec((SSM_CHUNK, SSM_WIDTH), lambda n, c: (n * nch + c, 0)), st_spec, st_spec],
        out_shape=[jax.ShapeDtypeStruct((n_seq * seq, SSM_WIDTH), F32),
                   jax.ShapeDtypeStruct((n_seq, 1, SSM_LANES), F32),
                   jax.ShapeDtypeStruct((n_seq, 1, SSM_LANES), F32)],
        scratch_shapes=[pltpu.VMEM((SSM_CHUNK, SSM_LANES), F32), pltpu.VMEM((SSM_CHUNK, SSM_LANES), F32),
                        pltpu.VMEM((1, SSM_LANES), F32), pltpu.VMEM((1, SSM_LANES), F32)],
        compiler_params=_cp(("parallel", "arbitrary")),
        name="ssm_prompt",
    )(u, pm, pmt, bbr, bbi, ar, ai, pwr, pwi, cc, d, wglu)


def _ssm_sample_kernel(u_ref, h0r_ref, h0i_ref, bbr_ref, bbi_ref, ar_ref, ai_ref, cc_ref, d_ref, wglu_ref,
                       a_ref, sr_ref, si_ref, hr_s, hi_s, *, nb, steps):
    u = u_ref[...]
    ub = u.astype(BF16)
    hr_s[...] = _dot(ub, bbr_ref[...])
    hi_s[...] = _dot(ub, bbi_ref[...])
    ar, ai = ar_ref[...], ai_ref[...]
    hr, hi = h0r_ref[...], h0i_ref[...]
    for t in range(steps):
        rows = pl.ds(t * nb, nb)
        hr, hi = ar * hr - ai * hi + hr_s[rows, :], ar * hi + ai * hr + hi_s[rows, :]
        hr_s[rows, :] = hr
        hi_s[rows, :] = hi
    sr_ref[...] = hr
    si_ref[...] = hi
    y = _ssm_tail(hr_s[...], hi_s[...], u, cc_ref, d_ref)
    a_ref[...] = _glu(jax.nn.gelu(y).astype(BF16), wglu_ref)


def _ssm_sample(u, h0r, h0i, consts, tp, nb, steps):
    _, _, bbr, bbi, ar, ai, _, _, cc, d, wglu = consts
    ts = nb * steps
    full = lambda a: pl.BlockSpec(a.shape, lambda i: tuple(0 for _ in a.shape))
    return pl.pallas_call(
        functools.partial(_ssm_sample_kernel, nb=nb, steps=steps),
        grid=(1,),
        in_specs=[pl.BlockSpec((ts, SSM_WIDTH), lambda i: (tp // ts, 0))]
        + [full(a) for a in (h0r, h0i, bbr, bbi, ar, ai, cc, d, wglu)],
        out_specs=[pl.BlockSpec((ts, SSM_WIDTH), lambda i: (0, 0)),
                   pl.BlockSpec((nb, SSM_LANES), lambda i: (0, 0)), pl.BlockSpec((nb, SSM_LANES), lambda i: (0, 0))],
        out_shape=[jax.ShapeDtypeStruct((ts, SSM_WIDTH), F32),
                   jax.ShapeDtypeStruct((nb, SSM_LANES), F32), jax.ShapeDtypeStruct((nb, SSM_LANES), F32)],
        scratch_shapes=[pltpu.VMEM((ts, SSM_LANES), F32), pltpu.VMEM((ts, SSM_LANES), F32)],
        compiler_params=_cp(("arbitrary",)),
        name="ssm_sample",
    )(u, h0r, h0i, bbr, bbi, ar, ai, cc, d, wglu)


def _expand_kv(x):
    lane = lax.broadcasted_iota(I32, x.shape, 1)
    low = lane < HEAD_DIM
    xr = pltpu.roll(x, HEAD_DIM, 1)
    return jnp.concatenate([jnp.where(low, x, xr), x, jnp.where(low, xr, x)], axis=1)


def _banded_kernel(q_ref, kp_ref, kc_ref, vp_ref, vc_ref, *rest, swa, rb):
    if swa:
        sink_ref, o_ref = rest
    else:
        o_ref, l_ref = rest
    nq = rb // BAND
    blk0 = pl.program_id(2) * nq
    kp, kc, vp, vc = kp_ref[...], kc_ref[...], vp_ref[...], vc_ref[...]
    if swa:
        kp, kc, vp, vc = _expand_kv(kp), _expand_kv(kc), _expand_kv(vp), _expand_kv(vc)
    kcat = jnp.concatenate([kp, kc], axis=0).astype(BF16)
    vcat = jnp.concatenate([vp, vc], axis=0).astype(BF16)
    q = q_ref[...].astype(BF16)
    qi = lax.broadcasted_iota(I32, (BAND, 2 * BAND), 0)
    ki = lax.broadcasted_iota(I32, (BAND, 2 * BAND), 1)
    dist = qi + BAND - ki
    in_band = (dist >= 0) & (dist <= BAND)
    head_in_slab = lax.broadcasted_iota(I32, (1, LANES), 1) // HEAD_DIM
    lane8 = lax.broadcasted_iota(I32, (BAND, LANES), 1)
    for jb in range(nq):
        valid = in_band & ((ki >= BAND) | (blk0 + jb > 0))
        kk = kcat[jb * BAND:jb * BAND + 2 * BAND]
        vv = vcat[jb * BAND:jb * BAND + 2 * BAND]
        slabs = []
        lse_tile = jnp.zeros((BAND, LANES), F32)
        for pr in range(ATT_WIDTH // LANES):
            cols = slice(pr * LANES, (pr + 1) * LANES)
            qs, ks, vs = q[jb * BAND:(jb + 1) * BAND, cols], kk[:, cols], vv[:, cols]
            o_pair = jnp.zeros((BAND, LANES), F32)
            for hh in range(LANES // HEAD_DIM):
                h = pr * 2 + hh
                hm = head_in_slab == hh
                s = _nt(jnp.where(hm, qs, jnp.zeros_like(qs)), ks)
                s = jnp.where(valid, s, NEG)
                m = jnp.max(s, -1, keepdims=True)
                e = jnp.exp(s - m)
                den = jnp.sum(e, -1, keepdims=True)
                o = _dot(e.astype(BF16), jnp.where(hm, vs, jnp.zeros_like(vs))) / den
                lse = m + jnp.log(den)
                if swa:
                    o = o * jax.nn.sigmoid(lse - sink_ref[:, h:h + 1])
                else:
                    lse_tile = jnp.where(lane8 == h, lse, lse_tile)
                o_pair = o_pair + o
            slabs.append(o_pair)
        rows = slice(jb * BAND, (jb + 1) * BAND)
        o_ref[rows, :] = jnp.concatenate(slabs, axis=1)
        if not swa:
            l_ref[rows, :] = lse_tile


def _banded(q, k, v, n_seq, seq, dil, sink=None):
    swa = sink is not None
    T = q.shape[0]
    wk = k.shape[1]
    sub = seq // dil
    rb = min(512, sub)
    nb = sub // rb
    nq = rb // BAND
    tp = n_seq * seq
    view = lambda a: a.reshape(T // dil, dil * a.shape[1])
    cur = lambda w: pl.BlockSpec((rb, w), lambda n, r, b: (n * nb + b, r))
    prev = lambda w: pl.BlockSpec(
        (BAND, w), lambda n, r, b: (jnp.maximum(n * nb * nq + b * nq - 1, n * nb * nq), r))
    in_specs = [cur(ATT_WIDTH), prev(wk), cur(wk), prev(wk), cur(wk)]
    args = [view(q), view(k), view(k), view(v), view(v)]
    out_specs = [cur(ATT_WIDTH)]
    out_shape = [jax.ShapeDtypeStruct((tp // dil, dil * ATT_WIDTH), F32)]
    if swa:
        in_specs.append(pl.BlockSpec((1, SUBLANES), lambda n, r, b: (0, 0)))
        args.append(sink)
    else:
        out_specs.append(cur(LANES))
        out_shape.append(jax.ShapeDtypeStruct((tp // dil, dil * LANES), F32))
    outs = pl.pallas_call(
        functools.partial(_banded_kernel, swa=swa, rb=rb),
        grid=(n_seq, dil, nb),
        in_specs=in_specs, out_specs=out_specs, out_shape=out_shape,
        compiler_params=_cp(("parallel", "parallel", "arbitrary")),
        name="swa_prompt" if swa else f"dil{dil}_prompt",
    )(*args)
    return [o.reshape(tp, -1) for o in outs]


def _stack_rows(rows):
    w = rows[0].shape[1]
    sub = lax.broadcasted_iota(I32, (SUBLANES, w), 0)
    out = jnp.zeros((SUBLANES, w), F32)
    for t, r in enumerate(rows):
        out = jnp.where(sub == t, r, out)
    return out


def _sample_attn_kernel(q_ref, kn_ref, vn_ref, kt_ref, vt_ref, *rest, swa, nbt, steps):
    if swa:
        sink_ref, o_ref = rest
    else:
        (o_ref,) = rest
    i = pl.program_id(0)
    ncol = steps * SUBLANES
    L = kt_ref.shape[2]
    head8 = (lax.broadcasted_iota(I32, (SUBLANES, ATT_WIDTH), 0)
             == lax.broadcasted_iota(I32, (SUBLANES, ATT_WIDTH), 1) // HEAD_DIM)
    r = lax.broadcasted_iota(I32, (ncol, L), 1)
    t = lax.broadcasted_iota(I32, (ncol, L), 0) // SUBLANES
    rn = lax.broadcasted_iota(I32, (ncol, SUBLANES), 1)
    tn = lax.broadcasted_iota(I32, (ncol, SUBLANES), 0) // SUBLANES
    one = lambda m: jnp.where(m, 1.0, 0.0)
    causal_new = one((rn <= tn) & (rn < steps))
    if swa:
        w = one(r >= t)
        wn = causal_new
    else:
        w = (one(r >= L - BAND + t) + one((r % 4 == t) & (r >= L - 4 * BAND)) + one(r % 16 == t))
        wn = causal_new + 2.0 * one(rn == tn)
    for b in range(nbt):
        idx = i * nbt + b
        kn = _stack_rows([kn_ref[s, pl.ds(idx, 1), :] for s in range(steps)])
        vn = _stack_rows([vn_ref[s, pl.ds(idx, 1), :] for s in range(steps)])
        kt, vt = kt_ref[b], vt_ref[b]
        if swa:
            kn, vn = _expand_kv(kn), _expand_kv(vn)
            rep = lambda a: jnp.concatenate([a[:HEAD_DIM]] * 3 + [a[HEAD_DIM:]] * 3, axis=0)
            kt, vt = rep(kt), rep(vt)
        qbd = jnp.concatenate(
            [jnp.where(head8, jnp.broadcast_to(q_ref[s, pl.ds(idx, 1), :], (SUBLANES, ATT_WIDTH)), 0.0)
             for s in range(steps)], axis=0).astype(BF16)
        sc = jnp.where(w > 0.0, _dot(qbd, kt.astype(BF16)), NEG)
        sn = jnp.where(wn > 0.0, _nt(qbd, kn.astype(BF16)), NEG)
        mx = jnp.maximum(jnp.max(sc, -1, keepdims=True), jnp.max(sn, -1, keepdims=True))
        p = jnp.exp(sc - mx) * w
        pn = jnp.exp(sn - mx) * wn
        den = jnp.sum(p, -1, keepdims=True) + jnp.sum(pn, -1, keepdims=True)
        o = (_nt(p.astype(BF16), vt.astype(BF16)) + _dot(pn.astype(BF16), vn.astype(BF16))) / den
        if swa:
            o = o * jax.nn.sigmoid(mx + jnp.log(den) - jnp.concatenate([sink_ref[...]] * steps, axis=0))
        for s in range(steps):
            rows = slice(s * SUBLANES, (s + 1) * SUBLANES)
            o_ref[s, pl.ds(idx, 1), :] = jnp.sum(jnp.where(head8, o[rows], 0.0), 0, keepdims=True)


def _sample_attn(q, k, v, cache_kt, cache_vt, layer, tp, nb, steps, sink=None):
    swa = sink is not None
    T = q.shape[0]
    nbt = 2
    blk0 = tp // nb // steps
    v3 = lambda a: a.reshape(T // nb, nb, a.shape[1])
    res = lambda w: pl.BlockSpec((steps, nb, w), lambda i: (blk0, 0, 0))
    base = layer * nb // nbt
    wk = k.shape[1]
    cspec = pl.BlockSpec((nbt,) + cache_kt.shape[1:], lambda i: (base + i, 0, 0))
    in_specs = [res(ATT_WIDTH), res(wk), res(wk), cspec, cspec]
    args = [v3(q), v3(k), v3(v), cache_kt, cache_vt]
    if swa:
        in_specs.append(pl.BlockSpec((SUBLANES, ATT_WIDTH), lambda i: (0, 0)))
        args.append(sink)
    out = pl.pallas_call(
        functools.partial(_sample_attn_kernel, swa=swa, nbt=nbt, steps=steps),
        grid=(nb // nbt,),
        in_specs=in_specs,
        out_specs=pl.BlockSpec((steps, nb, ATT_WIDTH), lambda i: (0, 0, 0)),
        out_shape=jax.ShapeDtypeStruct((steps, nb, ATT_WIDTH), F32),
        compiler_params=_cp(("arbitrary",)),
        name="swa_sample" if swa else "dil_sample",
    )(*args)
    return out.reshape(steps * nb, ATT_WIDTH)


def _mixout_kernel(*refs, merge):
    if merge:
        (x_ref, a_ref, o1_ref, o4_ref, o16_ref, l1_ref, l4_ref, l16_ref, oc_ref,
         wout_ref, g_ref, b_ref, wq_ref, x1_ref, qm_ref) = refs
        tm = x_ref.shape[0]
        lane_head = lax.broadcasted_iota(I32, (tm, ATT_WIDTH), 1) // HEAD_DIM
        ls = [l1_ref[...], l4_ref[...], l16_ref[...]]
        ws = [jnp.zeros((tm, ATT_WIDTH), F32) for _ in range(3)]
        for h in range(ATT_HEADS):
            lh = [l[:, h:h + 1] for l in ls]
            mx = jnp.maximum(jnp.maximum(lh[0], lh[1]), lh[2])
            eh = [jnp.exp(l - mx) for l in lh]
            tot = eh[0] + eh[1] + eh[2]
            for d in range(3):
                ws[d] = jnp.where(lane_head == h, eh[d] / tot, ws[d])
        ob = ws[0] * o1_ref[...] + ws[1] * o4_ref[...] + ws[2] * o16_ref[...]
    else:
        x_ref, a_ref, ob_ref, oc_ref, wout_ref, g_ref, b_ref, wq_ref, x1_ref, qm_ref = refs
        ob = ob_ref[...]
    cat = jnp.concatenate([a_ref[...].astype(BF16), ob.astype(BF16), oc_ref[...].astype(BF16)], axis=1)
    z = ALPHA * x_ref[...] + _dot(cat, wout_ref[...])
    x1 = _layer_norm(z, g_ref[...], b_ref[...])
    x1_ref[...] = x1
    qm_ref[...] = _dot(x1.astype(BF16), wq_ref[...]) * (MEM_HEAD_DIM ** -0.5)


def _mixout(x, acts, wout, g, b, wq, row0, nrows, tm, merge, alias=None):
    T = x.shape[0]
    nblk = nrows // tm
    off = row0 // tm
    xspec = lambda w: pl.BlockSpec((tm, w), lambda i: (off + i, 0))
    aspec = lambda a: pl.BlockSpec((tm, a.shape[1]), lambda i: (i, 0))
    wspec = lambda a: pl.BlockSpec(a.shape, lambda i: (0, 0))
    in_specs = [xspec(D_MODEL)] + [aspec(a) for a in acts] + [wspec(a) for a in (wout, g, b, wq)]
    args = [x] + list(acts) + [wout, g, b, wq]
    io_alias = {}
    if alias is not None:
        in_specs += [pl.BlockSpec(memory_space=pl.ANY)] * 2
        io_alias = {len(args): 0, len(args) + 1: 1}
        args += list(alias)
    kern = functools.partial(_mixout_kernel, merge=merge)
    if alias is not None:
        kern = functools.partial(_drop_alias_refs, kern, len(in_specs) - 2, 2)
    return pl.pallas_call(
        kern,
        grid=(nblk,),
        in_specs=in_specs,
        out_specs=[xspec(D_MODEL), xspec(D_MODEL)],
        out_shape=[jax.ShapeDtypeStruct((T, D_MODEL), F32)] * 2,
        input_output_aliases=io_alias,
        compiler_params=_cp(("parallel",)),
        name="mixout_prompt" if merge else "mixout_sample",
    )(*args)


def _drop_alias_refs(kern, first, count, *refs):
    return kern(*refs[:first], *refs[first + count:])


def _memkv_kernel(m_ref, wk_ref, wv_ref, k_ref, v_ref):
    m = m_ref[...].astype(BF16)
    k_ref[...] = _dot(m, wk_ref[...])
    v_ref[...] = _dot(m, wv_ref[...])


def _memkv(mem, wk, wv):
    n = mem.shape[0]
    rows = pl.BlockSpec((MEM_TOKENS, D_MODEL), lambda i: (i, 0))
    wspec = pl.BlockSpec((D_MODEL, D_MODEL), lambda i: (0, 0))
    return pl.pallas_call(
        _memkv_kernel,
        grid=(n // MEM_TOKENS,),
        in_specs=[rows, wspec, wspec],
        out_specs=[rows, rows],
        out_shape=[jax.ShapeDtypeStruct((n, D_MODEL), F32)] * 2,
        compiler_params=_cp(("parallel",)),
        name="memkv",
    )(mem, wk, wv)


def _memcore_prompt_kernel(q_ref, k_ref, v_ref, o_ref):
    q = q_ref[...].astype(BF16)
    k = k_ref[...].astype(BF16)
    v = v_ref[...].astype(BF16)
    outs = []
    for h in range(MEM_HEADS):
        cols = slice(h * MEM_HEAD_DIM, (h + 1) * MEM_HEAD_DIM)
        s = _nt(q[:, cols], k[:, cols])
        m = jnp.max(s, -1, keepdims=True)
        e = jnp.exp(s - m)
        den = jnp.sum(e, -1, keepdims=True)
        outs.append(_dot(e.astype(BF16), v[:, cols]) / den)
    o_ref[...] = jnp.concatenate(outs, axis=1)


def _memcore_prompt(qm, mk, mv, n_seq, seq, tm):
    T = qm.shape[0]
    per = seq // tm
    kv = pl.BlockSpec((MEM_TOKENS, D_MODEL), lambda i: (i // per, 0))
    rows = pl.BlockSpec((tm, D_MODEL), lambda i: (i, 0))
    return pl.pallas_call(
        _memcore_prompt_kernel,
        grid=(n_seq * per,),
        in_specs=[rows, kv, kv],
        out_specs=rows,
        out_shape=jax.ShapeDtypeStruct((T, D_MODEL), F32),
        compiler_params=_cp(("parallel",)),
        name="memcore_prompt",
    )(qm, mk, mv)


def _memcore_sample_kernel(q_ref, k_ref, v_ref, buf_ref, o_ref, *, nbt, steps):
    del buf_ref
    i = pl.program_id(0)
    ncol = steps * SUBLANES
    c = lax.broadcasted_iota(I32, (ncol, D_MODEL), 0)
    lane_head = lax.broadcasted_iota(I32, (ncol, D_MODEL), 1) // MEM_HEAD_DIM
    head_match = (c % SUBLANES) == lane_head
    expand = [jnp.where(head_match & ((c // SUBLANES) == t), 1.0, 0.0).astype(BF16) for t in range(steps)]
    for b in range(nbt):
        idx = i * nbt + b
        qbd = jnp.concatenate(
            [jnp.where(head_match[t * SUBLANES:(t + 1) * SUBLANES],
                       jnp.broadcast_to(q_ref[t, pl.ds(idx, 1), :], (SUBLANES, D_MODEL)), 0.0)
             for t in range(steps)], axis=0).astype(BF16)
        k = k_ref[b]
        v = v_ref[b]
        s = _nt(k.astype(BF16), qbd)
        mx = jnp.max(s, 0, keepdims=True)
        p = jnp.exp(s - mx).astype(BF16)
        for t in range(steps):
            pe = _dot(p, expand[t])
            o_ref[t, pl.ds(idx, 1), :] = jnp.sum(pe * v, 0, keepdims=True) / jnp.sum(pe, 0, keepdims=True)


def _memcore_sample(qm, cache_k, cache_v, buf, layer, tp, nb, steps):
    T = qm.shape[0]
    nbt = 4
    blk0 = tp // nb // steps
    base = layer * nb // nbt
    v3 = lambda a: a.reshape(T // nb, nb, D_MODEL)
    res = pl.BlockSpec((steps, nb, D_MODEL), lambda i: (blk0, 0, 0))
    cspec = pl.BlockSpec((nbt, MEM_TOKENS, D_MODEL), lambda i: (base + i, 0, 0))
    out = pl.pallas_call(
        functools.partial(_memcore_sample_kernel, nbt=nbt, steps=steps),
        grid=(nb // nbt,),
        in_specs=[res, cspec, cspec, pl.BlockSpec(memory_space=pl.ANY)],
        out_specs=res,
        out_shape=jax.ShapeDtypeStruct((T // nb, nb, D_MODEL), F32),
        input_output_aliases={3: 0},
        compiler_params=_cp(("arbitrary",)),
        name="memcore_sample",
    )(v3(qm), cache_k, cache_v, v3(buf))
    return out.reshape(T, D_MODEL)


def _memout_kernel(x1_ref, o_ref, wo_ref, g_ref, b_ref, wr_ref, br_ref, x2_ref, route_ref):
    z = ALPHA * x1_ref[...] + _dot(o_ref[...].astype(BF16), wo_ref[...])
    x2 = _layer_norm(z, g_ref[...], b_ref[...])
    x2_ref[...] = x2
    logits = jnp.dot(x2, wr_ref[...], preferred_element_type=F32, precision=lax.Precision.HIGHEST) + br_ref[...]
    lane = lax.broadcasted_iota(I32, logits.shape, 1)
    big = jnp.int32(LANES)
    lg = jnp.where((lane >= N_EXPERTS) & (lane < N_EXPERTS + N_GROUPS), logits, NEG)
    gmax = jnp.max(lg, -1, keepdims=True)
    gsel = jnp.min(jnp.where(lg == gmax, lane, big), -1, keepdims=True) - N_EXPERTS
    gwt = 1.0 / jnp.sum(jnp.exp(lg - gmax), -1, keepdims=True)
    le = jnp.where((lane < N_EXPERTS) & (lane // EXPERTS_PER_GROUP == gsel), logits, NEG)
    v1 = jnp.max(le, -1, keepdims=True)
    i1 = jnp.min(jnp.where(le == v1, lane, big), -1, keepdims=True)
    le2 = jnp.where(lane == i1, NEG, le)
    v2 = jnp.max(le2, -1, keepdims=True)
    i2 = jnp.min(jnp.where(le2 == v2, lane, big), -1, keepdims=True)
    e2 = jnp.exp(v2 - v1)
    w1 = gwt / (1.0 + e2)
    w2 = gwt * e2 / (1.0 + e2)
    l8 = lax.broadcasted_iota(I32, route_ref.shape, 1)
    route_ref[...] = jnp.where(l8 == 0, i1.astype(F32), jnp.where(l8 == 1, i2.astype(F32),
                               jnp.where(l8 == 2, w1, jnp.where(l8 == 3, w2, 0.0))))


def _memout(x1, o, wo, g, b, wr, br, tm):
    T = x1.shape[0]
    rows = lambda w: pl.BlockSpec((tm, w), lambda i: (i, 0))
    wspec = lambda a: pl.BlockSpec(a.shape, lambda i: (0, 0))
    return pl.pallas_call(
        _memout_kernel,
        grid=(T // tm,),
        in_specs=[rows(D_MODEL), rows(D_MODEL)] + [wspec(a) for a in (wo, g, b, wr, br)],
        out_specs=[rows(D_MODEL), rows(SUBLANES)],
        out_shape=[jax.ShapeDtypeStruct((T, D_MODEL), F32), jax.ShapeDtypeStruct((T, SUBLANES), F32)],
        compiler_params=_cp(("parallel",)),
        name="memout_router",
    )(x1, o, wo, g, b, wr, br)


MOE_ROWS = 256


def _moe_kernel(tok_ref, gate_ref, offs_ref, x_ref, wg_ref, wu_ref, wd_ref, g_ref, b_ref, o_ref, xs_ref, ys_ref,
                *, tc):
    c = pl.program_id(0)
    e = pl.program_id(1)
    ntile = tc // MOE_ROWS if tc % MOE_ROWS == 0 else None
    rt = MOE_ROWS if ntile is not None else SUBLANES * 16
    nrt = tc // rt

    @pl.when(e == 0)
    def _():
        def zero(i, _):
            o_ref[pl.ds(pl.multiple_of(i * rt, rt), rt), :] = jnp.zeros((rt, D_MODEL), F32)
            return 0
        lax.fori_loop(0, nrt, zero, 0)

    @pl.when((e == 0) & (c == 0))
    def _():
        xs_ref[...] = jnp.zeros_like(xs_ref)

    start = offs_ref[c * (N_EXPERTS + 1) + e]
    n = offs_ref[c * (N_EXPERTS + 1) + e + 1] - start
    base = c * 2 * tc + start

    def tile(j, _):
        r0 = j * MOE_ROWS
        cnt = jnp.minimum(n - r0, MOE_ROWS)

        def gather(r, _):
            t = tok_ref[base + r0 + r]
            xs_ref[pl.ds(r, 1), :] = x_ref[pl.ds(t, 1), :]
            return 0
        lax.fori_loop(0, cnt, gather, 0)
        xb = xs_ref[...].astype(BF16)
        hid = jax.nn.silu(_dot(xb, wg_ref[0])) * _dot(xb, wu_ref[0])
        ys_ref[...] = _dot(hid.astype(BF16), wd_ref[0])

        def scatter(r, _):
            t = tok_ref[base + r0 + r]
            gt = lax.bitcast_convert_type(gate_ref[base + r0 + r], F32)
            o_ref[pl.ds(t, 1), :] = o_ref[pl.ds(t, 1), :] + gt * ys_ref[pl.ds(r, 1), :]
            return 0
        lax.fori_loop(0, cnt, scatter, 0)
        return 0

    lax.fori_loop(0, (n + MOE_ROWS - 1) // MOE_ROWS, tile, 0)

    @pl.when(e == N_EXPERTS - 1)
    def _():
        g = g_ref[...]
        b = b_ref[...]

        def norm(i, _):
            rows = pl.ds(pl.multiple_of(i * rt, rt), rt)
            o_ref[rows, :] = _layer_norm(ALPHA * x_ref[rows, :] + o_ref[rows, :], g, b)
            return 0
        lax.fori_loop(0, nrt, norm, 0)


def _moe(x2, route, wg, wu, wd, g, b, nc):
    T = x2.shape[0]
    tc = T // nc
    eid = route[:, 0:2].astype(I32).reshape(nc, 2 * tc)
    ew = route[:, 2:4].reshape(nc, 2 * tc)
    pair = lax.broadcasted_iota(I32, eid.shape, 1)
    _, order, gate = lax.sort((eid, pair, lax.bitcast_convert_type(ew, I32)), dimension=1, num_keys=1, is_stable=True)
    tok = (order // 2).reshape(-1)
    gate = gate.reshape(-1)
    counts = jnp.sum(jax.nn.one_hot(eid, N_EXPERTS, dtype=I32), axis=1)
    offs = jnp.concatenate([jnp.zeros((nc, 1), I32), jnp.cumsum(counts, axis=1).astype(I32)], axis=1).reshape(-1)
    chunk = pl.BlockSpec((tc, D_MODEL), lambda c, e, *_: (c, 0), pipeline_mode=pl.Buffered(1))
    wspec = lambda k, n: pl.BlockSpec((1, k, n), lambda c, e, *_: (e, 0, 0))
    vec = pl.BlockSpec((1, D_MODEL), lambda c, e, *_: (0, 0))
    return pl.pallas_call(
        functools.partial(_moe_kernel, tc=tc),
        grid_spec=pltpu.PrefetchScalarGridSpec(
            num_scalar_prefetch=3,
            grid=(nc, N_EXPERTS),
            in_specs=[chunk, wspec(D_MODEL, EXPERT_FF), wspec(D_MODEL, EXPERT_FF), wspec(EXPERT_FF, D_MODEL), vec, vec],
            out_specs=chunk,
            scratch_shapes=[pltpu.VMEM((MOE_ROWS, D_MODEL), F32), pltpu.VMEM((MOE_ROWS, D_MODEL), F32)],
        ),
        out_shape=jax.ShapeDtypeStruct((T, D_MODEL), F32),
        compiler_params=_cp(("parallel", "arbitrary")),
        name="moe",
    )(tok, gate, offs, x2, wg, wu, wd, g, b)


def _rope_tables(n_seq, seq, nb, steps):
    half = HEAD_DIM // 2
    inv_freq = ROPE_THETA ** (-jnp.arange(half, dtype=F32) / half)
    pos = jnp.concatenate([jnp.tile(jnp.arange(seq), n_seq), PAST_LEN + jnp.repeat(jnp.arange(steps), nb)])
    ang = pos.astype(F32)[:, None] * inv_freq[None, :]
    cos, sin = jnp.cos(ang), jnp.sin(ang)
    return jnp.tile(cos, (1, LANES // half)), jnp.concatenate([-sin, sin] * (LANES // HEAD_DIM), axis=1)


def _segment_permutation():
    pm = np.zeros((SSM_CHUNK, SSM_CHUNK), np.float32)
    for j in range(SUBLANES):
        for i in range(SSM_SEG):
            pm[i * SUBLANES + j, SSM_SEG * j + i] = 1.0
    return jnp.asarray(pm, BF16), jnp.asarray(pm.T, BF16)


def kernel(x_prompt, x_sample, mem_prompt, cache_dil_k, cache_dil_v, cache_swa_k, cache_swa_v, state_ssm_re, state_ssm_im, cache_mem_k, cache_mem_v, w_in, ssm_lambda_re, ssm_lambda_im, ssm_b_re, ssm_b_im, ssm_c_re, ssm_c_im, ssm_d, ssm_log_dt, w_glu, swa_sink, w_out, w_mem_q, w_mem_k, w_mem_v, w_mem_o, ln_gain, ln_bias, w_route_group, b_route_group, w_route_expert, b_route_expert, w_exp_gate, w_exp_up, w_exp_down):
    n_seq, seq, _ = x_prompt.shape
    nb, steps, _ = x_sample.shape
    tp, ts = n_seq * seq, nb * steps
    T = tp + ts
    tm = 512
    nc = 4 if T % (4 * LANES) == 0 and T // 4 >= 4096 else 1
    assert tp % tm == 0 and ts % tm == 0 and seq % DIL_REACH == 0 and T % nc == 0
    assert cache_dil_k.shape[2] == DIL_REACH and cache_swa_k.shape[2] == SWA_WINDOW

    x = jnp.concatenate([x_prompt.reshape(tp, D_MODEL), x_sample.transpose(1, 0, 2).reshape(ts, D_MODEL)])
    cos, sin = _rope_tables(n_seq, seq, nb, steps)
    pm, pmt = _segment_permutation()
    mem = mem_prompt.reshape(n_seq * MEM_TOKENS, D_MODEL)
    feat_major = lambda a: a.transpose(0, 1, 3, 4, 2).reshape(DEPTH * nb, a.shape[3] * HEAD_DIM, a.shape[2])
    cdk, cdv, csk, csv = (feat_major(a) for a in (cache_dil_k, cache_dil_v, cache_swa_k, cache_swa_v))
    cmk = cache_mem_k.reshape(DEPTH * nb, MEM_TOKENS, D_MODEL)
    cmv = cache_mem_v.reshape(DEPTH * nb, MEM_TOKENS, D_MODEL)

    outs = {k: [] for k in ("pbk", "pbv", "pck", "pcv", "psr", "psi", "pmk", "pmv", "sbk", "sbv", "sck", "scv", "ssr", "ssi")}
    for l in range(DEPTH):
        bf = lambda a: a[l].astype(BF16)
        u, qb, kb, vb, qc, kc, vc = _inproj(x, bf(w_in), cos, sin, tm)

        ar, ai, pwr, pwi, bbr, bbi = _ssm_prep(ssm_lambda_re[l], ssm_lambda_im[l], ssm_log_dt[l],
                                               ssm_b_re[l], ssm_b_im[l])
        cc = jnp.concatenate([_blockdiag_out(ssm_c_re[l]), -_blockdiag_out(ssm_c_im[l])], axis=0).astype(BF16)
        consts = (pm, pmt, bbr, bbi, ar, ai, pwr, pwi, cc, ssm_d[l].reshape(1, SSM_WIDTH), bf(w_glu))
        a_p, psr, psi = _ssm_prompt(u, consts, n_seq, seq)
        a_s, ssr, ssi = _ssm_sample(u, state_ssm_re[l].reshape(nb, SSM_LANES), state_ssm_im[l].reshape(nb, SSM_LANES),
                                    consts, tp, nb, steps)

        sink8 = jnp.pad(swa_sink[l], (0, SUBLANES - ATT_HEADS)).reshape(1, SUBLANES)
        sink_rows = jnp.broadcast_to(sink8.reshape(SUBLANES, 1), (SUBLANES, ATT_WIDTH))
        branches = [_banded(qb, kb, vb, n_seq, seq, d) for d in DILATIONS]
        (oc_p,) = _banded(qc, kc, vc, n_seq, seq, 1, sink=sink8)
        ob_s = _sample_attn(qb, kb, vb, cdk, cdv, l, tp, nb, steps)
        oc_s = _sample_attn(qc, kc, vc, csk, csv, l, tp, nb, steps, sink=sink_rows)

        g = lambda i: ln_gain[l, i].reshape(1, D_MODEL)
        b = lambda i: ln_bias[l, i].reshape(1, D_MODEL)
        acts_p = [a_p] + [o for o, _ in branches] + [ls for _, ls in branches] + [oc_p]
        x1, qm = _mixout(x, acts_p, bf(w_out), g(0), b(0), bf(w_mem_q), 0, tp, tm, True)
        x1, qm = _mixout(x, [a_s, ob_s, oc_s], bf(w_out), g(0), b(0), bf(w_mem_q), tp, ts, tm, False, alias=(x1, qm))

        mk, mv = _memkv(mem, bf(w_mem_k), bf(w_mem_v))
        om = _memcore_prompt(qm, mk, mv, n_seq, seq, tm)
        om = _memcore_sample(qm, cmk, cmv, om, l, tp, nb, steps)
        wr = jnp.zeros((D_MODEL, LANES), F32).at[:, :N_EXPERTS].set(w_route_expert[l])
        wr = wr.at[:, N_EXPERTS:N_EXPERTS + N_GROUPS].set(w_route_group[l])
        br = jnp.zeros((1, LANES), F32).at[0, :N_EXPERTS].set(b_route_expert[l])
        br = br.at[0, N_EXPERTS:N_EXPERTS + N_GROUPS].set(b_route_group[l])
        x2, route = _memout(x1, om, bf(w_mem_o), g(1), b(1), wr, br, tm)

        x = _moe(x2, route, bf(w_exp_gate), bf(w_exp_up), bf(w_exp_down), g(2), b(2), nc)

        keep_b, keep_c = min(DIL_REACH, seq), min(SWA_WINDOW, seq)
        p4 = lambda a, w, keep: a[:tp].reshape(n_seq, seq, w // HEAD_DIM, HEAD_DIM)[:, seq - keep:]
        s4 = lambda a, w: a[tp:].reshape(steps, nb, w // HEAD_DIM, HEAD_DIM).transpose(1, 0, 2, 3)
        outs["pbk"].append(p4(kb, ATT_WIDTH, keep_b)); outs["pbv"].append(p4(vb, ATT_WIDTH, keep_b))
        outs["pck"].append(p4(kc, SWA_KV_WIDTH, keep_c)); outs["pcv"].append(p4(vc, SWA_KV_WIDTH, keep_c))
        outs["psr"].append(psr.reshape(n_seq, SSM_GROUPS, SSM_STATE)); outs["psi"].append(psi.reshape(n_seq, SSM_GROUPS, SSM_STATE))
        outs["pmk"].append(mk.reshape(n_seq, MEM_TOKENS, MEM_HEADS, MEM_HEAD_DIM))
        outs["pmv"].append(mv.reshape(n_seq, MEM_TOKENS, MEM_HEADS, MEM_HEAD_DIM))
        outs["sbk"].append(s4(kb, ATT_WIDTH)); outs["sbv"].append(s4(vb, ATT_WIDTH))
        outs["sck"].append(s4(kc, SWA_KV_WIDTH)); outs["scv"].append(s4(vc, SWA_KV_WIDTH))
        outs["ssr"].append(ssr.reshape(nb, SSM_GROUPS, SSM_STATE)); outs["ssi"].append(ssi.reshape(nb, SSM_GROUPS, SSM_STATE))

    yp = x[:tp].reshape(n_seq, seq, D_MODEL)
    ys = x[tp:].reshape(steps, nb, D_MODEL).transpose(1, 0, 2)
    st = lambda k: jnp.stack(outs[k])
    return (yp, ys, st("pbk"), st("pbv"), st("pck"), st("pcv"), st("psr"), st("psi"), st("pmk"), st("pmv"),
            st("sbk"), st("sbv"), st("sck"), st("scv"), st("ssr"), st("ssi"))
```
